```python
import jax, jax.numpy as jnp
from jax import lax
import numpy as np

D_MODEL = 2048
BATCH = 16
SEQ = 256
DEPTH = 2
DEC_BATCH = 2
DEC_SEQ = 2048
PAST_LEN = 256

GRID_W = 64
N_BRANCH = 4
BRANCH_W = 1024
HEAD_DIM = 128
N_HEADS = BRANCH_W // HEAD_DIM
N_KV_HEADS = 2
GQA_GROUP = N_HEADS // N_KV_HEADS
Q_COLS = N_HEADS * HEAD_DIM
KV_COLS = N_KV_HEADS * HEAD_DIM
WINDOW = 128
Q_BLOCK = 128
ROPE_THETA = 10000.0
CONV_A_K = 31
CONV_D_K = 3
N_FOURIER_GROUPS = 4
FOURIER_GROUP_W = BRANCH_W // N_FOURIER_GROUPS
IN_COLS = 2 * BRANCH_W + Q_COLS + 2 * KV_COLS + BRANCH_W + 3 * BRANCH_W
D_FF = 5632
N_EXPERTS = 8
TOP_K = 2
D_FF_EXPERT = 7168
N_DENSE = (DEPTH + 1) // 2
N_MOE = DEPTH // 2
EPS = 1e-6
NEG_BIG = -1e30

kernel_name = "hybrid_diffusion_gated_branches_step"


def _rms(x, g):
    xf = x.astype(jnp.float32)
    y = xf * lax.rsqrt(jnp.mean(xf * xf, axis=-1, keepdims=True) + EPS)
    return (y * g.astype(jnp.float32)).astype(x.dtype)


def _layernorm(x, g, b):
    xf = x.astype(jnp.float32)
    mu = jnp.mean(xf, axis=-1, keepdims=True)
    xc = xf - mu
    var = jnp.mean(xc * xc, axis=-1, keepdims=True)
    return (xc * lax.rsqrt(var + EPS) * g.astype(jnp.float32) + b.astype(jnp.float32)).astype(x.dtype)


def _dwconv(x, w, pad):
    return lax.conv_general_dilated(
        x, w.astype(x.dtype)[:, None, :], window_strides=(1,), padding=[(pad, pad)],
        dimension_numbers=("NWC", "WIO", "NWC"), feature_group_count=x.shape[-1])


def _axial_rope(x):
    s = x.shape[1]
    n_rows = s // GRID_W
    row = jnp.repeat(jnp.arange(n_rows, dtype=jnp.float32), GRID_W)
    col = jnp.tile(jnp.arange(GRID_W, dtype=jnp.float32), n_rows)
    nf = HEAD_DIM // 4
    inv = ROPE_THETA ** (-jnp.arange(nf, dtype=jnp.float32) / nf)
    ang = jnp.stack([row[:, None] * inv, col[:, None] * inv], axis=1)
    ang = ang.reshape((1, s) + (1,) * (x.ndim - 3) + (2, nf))
    cos, sin = jnp.cos(ang), jnp.sin(ang)
    xr = x.astype(jnp.float32).reshape(x.shape[:-1] + (2, 2, nf))
    x1, x2 = xr[..., 0, :], xr[..., 1, :]
    out = jnp.stack([x1 * cos - x2 * sin, x1 * sin + x2 * cos], axis=-2)
    return out.reshape(x.shape).astype(x.dtype)


def _attend(q, k, v, mask, sink):
    s = jnp.einsum("bqhgd,bkhd->bhgqk", q, k, preferred_element_type=jnp.float32) * (HEAD_DIM ** -0.5)
    if mask is not None:
        s = jnp.where(mask, s, NEG_BIG)
    sink_col = jnp.broadcast_to(sink.astype(jnp.float32).reshape(N_KV_HEADS, GQA_GROUP, 1, 1),
                                s.shape[:-1] + (1,))
    p = jax.nn.softmax(jnp.concatenate([s, sink_col], axis=-1), axis=-1)[..., :-1]
    return jnp.einsum("bhgqk,bkhd->bqhgd", p.astype(v.dtype), v)


def _context_attention(q, k, v, sink):
    b, s = q.shape[:2]
    nb = s // Q_BLOCK
    qb = jnp.moveaxis(q.reshape((b, nb, Q_BLOCK) + q.shape[2:]), 1, 0)
    ob = lax.map(lambda qi: _attend(qi, k, v, None, sink), qb)
    return jnp.moveaxis(ob, 0, 1).reshape(q.shape)


def _latent_attention(q, k, v, k_ctx, v_ctx, sink):
    b, s = q.shape[:2]
    nb = s // Q_BLOCK
    span = Q_BLOCK + 2 * WINDOW
    pad = [(0, 0), (WINDOW, WINDOW), (0, 0), (0, 0)]
    kp, vp = jnp.pad(k, pad), jnp.pad(v, pad)
    qb = jnp.moveaxis(q.reshape((b, nb, Q_BLOCK) + q.shape[2:]), 1, 0)
    ctx_mask = jnp.ones((Q_BLOCK, k_ctx.shape[1]), dtype=bool)

    def block(args):
        i, qi = args
        start = i * Q_BLOCK
        kw = lax.dynamic_slice_in_dim(kp, start, span, axis=1)
        vw = lax.dynamic_slice_in_dim(vp, start, span, axis=1)
        qpos = start + jnp.arange(Q_BLOCK)
        kpos = start - WINDOW + jnp.arange(span)
        rel = kpos[None, :] - qpos[:, None]
        win = (jnp.abs(rel) <= WINDOW) & (kpos >= 0)[None, :] & (kpos < s)[None, :]
        mask = jnp.concatenate([win, ctx_mask], axis=1)
        return _attend(qi, jnp.concatenate([kw, k_ctx], axis=1), jnp.concatenate([vw, v_ctx], axis=1), mask, sink)

    ob = lax.map(block, (jnp.arange(nb), qb))
    return jnp.moveaxis(ob, 0, 1).reshape(q.shape)


def _mixer(h, ctx_k, ctx_v, w_in, conv_a_w, conv_a_b, ln_a_g, ln_a_b, q_norm_g, k_norm_g,
           attn_sink, conv_d_w, w_branch, w_gate, b_gate, w_o):
    b, s, _ = h.shape
    sizes = [BRANCH_W, BRANCH_W, Q_COLS, KV_COLS, KV_COLS, BRANCH_W, BRANCH_W, BRANCH_W, BRANCH_W]
    offsets = np.cumsum(sizes)[:-1].tolist()
    proj = jnp.einsum("bsd,dc->bsc", h, w_in)
    a_val, a_gate, q, k, v, f, d_x, d_b, d_c = jnp.split(proj, offsets, axis=-1)

    a = a_val * jax.nn.sigmoid(a_gate)
    a = _dwconv(a, conv_a_w, CONV_A_K // 2) + conv_a_b
    a = jax.nn.silu(_layernorm(a, ln_a_g, ln_a_b))

    q = _rms(q.reshape(b, s, N_KV_HEADS, GQA_GROUP, HEAD_DIM), q_norm_g)
    k = _rms(k.reshape(b, s, N_KV_HEADS, HEAD_DIM), k_norm_g)
    v = v.reshape(b, s, N_KV_HEADS, HEAD_DIM)
    if ctx_k is None:
        att = _context_attention(q, k, v, attn_sink)
    else:
        att = _latent_attention(_axial_rope(q), _axial_rope(k), v, ctx_k, ctx_v, attn_sink)
    att = att.reshape(b, s, Q_COLS)

    ff = f.astype(jnp.float32).reshape(b, s, N_FOURIER_GROUPS, FOURIER_GROUP_W)
    four = jnp.fft.fft2(ff, axes=(1, 3), norm="ortho").real.astype(h.dtype).reshape(b, s, BRANCH_W)

    dy = d_b * _dwconv(d_c * d_x, conv_d_w, CONV_D_K // 2)

    branches = jnp.stack([a, att, four, dy], axis=2)
    proj_b = jnp.einsum("bsnc,ncd->bsnd", branches, w_branch)
    gates = jax.nn.sigmoid(jnp.einsum("bsd,de->bse", h, w_gate) + b_gate).reshape(b, s, N_BRANCH, D_MODEL)
    merged = jnp.sum(gates * proj_b, axis=2)
    return jnp.einsum("bsd,de->bse", merged, w_o), k, v


def _swiglu(h, w1, w3, w2):
    return jnp.einsum("bsf,fd->bsd", jax.nn.silu(h @ w1) * (h @ w3), w2)


def _moe(h, router_w, router_b, w1, w3, w2):
    b, s, d = h.shape
    t = h.reshape(b * s, d)
    logits = t.astype(jnp.float32) @ router_w.astype(jnp.float32) + router_b.astype(jnp.float32)
    top_v, top_i = lax.top_k(logits, TOP_K)
    top_p = jax.nn.softmax(top_v, axis=-1)
    combine = jnp.sum(jax.nn.one_hot(top_i, N_EXPERTS, dtype=jnp.float32) * top_p[..., None], axis=1)
    out = jnp.zeros_like(t)
    for e in range(N_EXPERTS):
        y = (jax.nn.silu(t @ w1[e]) * (t @ w3[e])) @ w2[e]
        out = out + combine[:, e:e + 1].astype(t.dtype) * y
    return out.reshape(b, s, d)


def _layer(x, c_act, ctx_k, ctx_v, ada_w, ada_b, norm_mix_g, norm_ffn_g, mix_w, ffn_w, is_moe):
    mod = (jnp.einsum("bd,de->be", c_act, ada_w) + ada_b)[:, None, :]
    sh1, sc1, g1, sh2, sc2, g2 = jnp.split(mod, 6, axis=-1)
    h = _rms(x, norm_mix_g) * (1 + sc1) + sh1
    mix_out, k, v = _mixer(h, ctx_k, ctx_v, *mix_w)
    x = x + g1 * mix_out
    h = _rms(x, norm_ffn_g) * (1 + sc2) + sh2
    ffn_out = _moe(h, *ffn_w) if is_moe else _swiglu(h, *ffn_w)
    x = x + g2 * ffn_out
    return x, k, v


def setup_inputs(seed: int = 0) -> dict:
    key = jax.random.key(seed)
    ks = jax.random.split(key, 40)
    nrm = jax.random.normal
    f32 = jnp.float32
    D = D_MODEL
    return {
        "x_prompt": nrm(ks[0], (BATCH, SEQ, D), f32),
        "x_sample": nrm(ks[1], (DEC_BATCH, DEC_SEQ, D), f32),
        "c": nrm(ks[2], (DEC_BATCH, D), f32),
        "cache_k": nrm(ks[3], (DEC_BATCH, DEPTH, PAST_LEN, N_KV_HEADS, HEAD_DIM), f32),
        "cache_v": nrm(ks[4], (DEC_BATCH, DEPTH, PAST_LEN, N_KV_HEADS, HEAD_DIM), f32),
        "c_ctx": nrm(ks[5], (D,), f32),
        "ada_w": nrm(ks[6], (DEPTH, D, 6 * D), f32) * (0.5 * D ** -0.5),
        "ada_b": nrm(ks[7], (DEPTH, 6 * D), f32) * 0.01,
        "norm_mix_g": 1.0 + 0.02 * nrm(ks[8], (DEPTH, D), f32),
        "norm_ffn_g": 1.0 + 0.02 * nrm(ks[9], (DEPTH, D), f32),
        "w_in": nrm(ks[10], (DEPTH, D, IN_COLS), f32) * D ** -0.5,
        "conv_a_w": nrm(ks[11], (DEPTH, CONV_A_K, BRANCH_W), f32) * CONV_A_K ** -0.5,
        "conv_a_b": nrm(ks[12], (DEPTH, BRANCH_W), f32) * 0.01,
        "ln_a_g": 1.0 + 0.02 * nrm(ks[13], (DEPTH, BRANCH_W), f32),
        "ln_a_b": nrm(ks[14], (DEPTH, BRANCH_W), f32) * 0.01,
        "q_norm_g": 1.0 + 0.02 * nrm(ks[15], (DEPTH, HEAD_DIM), f32),
        "k_norm_g": 1.0 + 0.02 * nrm(ks[16], (DEPTH, HEAD_DIM), f32),
        "attn_sink": nrm(ks[17], (DEPTH, N_HEADS), f32),
        "conv_d_w": nrm(ks[18], (DEPTH, CONV_D_K, BRANCH_W), f32) * CONV_D_K ** -0.5,
        "w_branch": nrm(ks[19], (DEPTH, N_BRANCH, BRANCH_W, D), f32) * BRANCH_W ** -0.5,
        "w_gate": nrm(ks[20], (DEPTH, D, N_BRANCH * D), f32) * D ** -0.5,
        "b_gate": nrm(ks[21], (DEPTH, N_BRANCH * D), f32) * 0.01,
        "w_o": nrm(ks[22], (DEPTH, D, D), f32) * D ** -0.5,
        "ffn_w1": nrm(ks[23], (N_DENSE, D, D_FF), f32) * D ** -0.5,
        "ffn_w3": nrm(ks[24], (N_DENSE, D, D_FF), f32) * D ** -0.5,
        "ffn_w2": nrm(ks[25], (N_DENSE, D_FF, D), f32) * D_FF ** -0.5,
        "router_w": nrm(ks[26], (N_MOE, D, N_EXPERTS), f32) * D ** -0.5,
        "router_b": nrm(ks[27], (N_MOE, N_EXPERTS), f32) * 0.01,
        "moe_w1": nrm(ks[28], (N_MOE, N_EXPERTS, D, D_FF_EXPERT), f32) * D ** -0.5,
        "moe_w3": nrm(ks[29], (N_MOE, N_EXPERTS, D, D_FF_EXPERT), f32) * D ** -0.5,
        "moe_w2": nrm(ks[30], (N_MOE, N_EXPERTS, D_FF_EXPERT, D), f32) * D_FF_EXPERT ** -0.5,
    }


def reference(x_prompt, x_sample, c, cache_k, cache_v, c_ctx, ada_w, ada_b, norm_mix_g, norm_ffn_g,
              w_in, conv_a_w, conv_a_b, ln_a_g, ln_a_b, q_norm_g, k_norm_g, attn_sink, conv_d_w,
              w_branch, w_gate, b_gate, w_o, ffn_w1, ffn_w3, ffn_w2, router_w, router_b,
              moe_w1, moe_w3, moe_w2):
    c_ctx_act = jax.nn.silu(c_ctx)[None, :]
    c_act = jax.nn.silu(c)
    y_prompt, y_sample = x_prompt, x_sample
    new_k, new_v = [], []
    for l in range(DEPTH):
        mix_w = (w_in[l], conv_a_w[l], conv_a_b[l], ln_a_g[l], ln_a_b[l], q_norm_g[l], k_norm_g[l],
                 attn_sink[l], conv_d_w[l], w_branch[l], w_gate[l], b_gate[l], w_o[l])
        j = l // 2
        is_moe = (l % 2) == 1
        if is_moe:
            ffn_w = (router_w[j], router_b[j], moe_w1[j], moe_w3[j], moe_w2[j])
        else:
            ffn_w = (ffn_w1[j], ffn_w3[j], ffn_w2[j])
        y_prompt, k_l, v_l = _layer(y_prompt, c_ctx_act, None, None, ada_w[l], ada_b[l],
                                    norm_mix_g[l], norm_ffn_g[l], mix_w, ffn_w, is_moe)
        new_k.append(k_l)
        new_v.append(v_l)
        y_sample, _, _ = _layer(y_sample, c_act, cache_k[:, l], cache_v[:, l], ada_w[l], ada_b[l],
                                norm_mix_g[l], norm_ffn_g[l], mix_w, ffn_w, is_moe)
    new_cache_k = jnp.stack(new_k, axis=1)
    new_cache_v = jnp.stack(new_v, axis=1)
    return (y_prompt, y_sample, new_cache_k, new_cache_v)
```

```python
import functools

import jax
import jax.numpy as jnp
import numpy as np
from jax import lax
from jax.experimental import pallas as pl
from jax.experimental.pallas import tpu as pltpu

F32 = jnp.float32
BF16 = jnp.bfloat16

GRID_W = 64
WINDOW = 128
ROPE_THETA = 10000.0
N_FOURIER_GROUPS = 4
TOP_K = 2
EPS = 1e-6
NEG_BIG = -1e30

V7X_VMEM_BYTES = 64 * 1024 * 1024
VMEM_LIMIT_BYTES = V7X_VMEM_BYTES - 8 * 1024 * 1024
BF16_SUBLANES = 16
LANES = 128

ROW_TILE = 256
MOE_TILE = 256
HALO = BF16_SUBLANES


def _params(*sem):
    return pltpu.CompilerParams(dimension_semantics=sem, vmem_limit_bytes=VMEM_LIMIT_BYTES)


def _sigmoid(x):
    return 1.0 / (1.0 + jnp.exp(-x))


def _silu(x):
    return x * _sigmoid(x)


def _segment(row0, mp, ds):
    return jnp.where(row0 < mp, 0, 1 + (row0 - mp) // ds)


def _adaln_kernel(c_ref, w_ref, b_ref, o_ref):
    a = _silu(c_ref[...])
    o_ref[...] = jnp.dot(a, w_ref[...], preferred_element_type=F32,
                         precision=lax.Precision.HIGHEST) + b_ref[...]


def _adaln(cvec, ada_w, ada_b):
    depth, d, n = ada_w.shape
    rows = cvec.shape[0]
    tn = 1024
    return pl.pallas_call(
        _adaln_kernel,
        grid=(depth, n // tn),
        in_specs=[
            pl.BlockSpec((rows, d), lambda l, j: (0, 0)),
            pl.BlockSpec((None, d, tn), lambda l, j: (l, 0, j)),
            pl.BlockSpec((None, 1, tn), lambda l, j: (l, 0, j)),
        ],
        out_specs=pl.BlockSpec((None, rows, tn), lambda l, j: (l, 0, j)),
        out_shape=jax.ShapeDtypeStruct((depth, rows, n), F32),
        compiler_params=_params("arbitrary", "arbitrary"),
        name="adaln",
    )(cvec, ada_w, ada_b.reshape(depth, 1, n))


def _norm_mod_kernel(x_ref, g_ref, mod_ref, o_ref, *, sh_idx, sc_idx):
    x = x_ref[...]
    ms = jnp.mean(x * x, axis=-1, keepdims=True)
    y = x * lax.rsqrt(ms + EPS) * g_ref[...]
    sc = mod_ref[sc_idx:sc_idx + 1, :]
    sh = mod_ref[sh_idx:sh_idx + 1, :]
    o_ref[...] = (y * (1.0 + sc) + sh).astype(o_ref.dtype)


def _norm_mod(x, g, mod3, sh_idx, sc_idx, out_dtype, mp, ds):
    m, d = x.shape
    tm = ROW_TILE
    return pl.pallas_call(
        functools.partial(_norm_mod_kernel, sh_idx=sh_idx, sc_idx=sc_idx),
        grid=(m // tm,),
        in_specs=[
            pl.BlockSpec((tm, d), lambda i: (i, 0)),
            pl.BlockSpec((1, d), lambda i: (0, 0)),
            pl.BlockSpec((None, 6, d), lambda i: (_segment(i * tm, mp, ds), 0, 0)),
        ],
        out_specs=pl.BlockSpec((tm, d), lambda i: (i, 0)),
        out_shape=jax.ShapeDtypeStruct((m, d), out_dtype),
        compiler_params=_params("arbitrary"),
        name="norm_mod",
    )(x, g.reshape(1, d), mod3)


def _cast_weight(w_ref, wb_ref):
    k = w_ref.shape[0]
    kc = min(k, 512)
    for k0 in range(0, k, kc):
        wb_ref[k0:k0 + kc, :] = w_ref[k0:k0 + kc, :].astype(BF16)


def _gmm_kernel(tg_ref, nt_ref, *refs, epilogue, g_idx):
    t = pl.program_id(1)
    valid = t < nt_ref[0]
    first = jnp.logical_or(t == 0, tg_ref[t] != tg_ref[jnp.maximum(t - 1, 0)])
    n_weights = 2 if epilogue == "swiglu" else 1
    o_ref = refs[-1 - n_weights]

    @pl.when(jnp.logical_not(valid))
    def _():
        o_ref[...] = jnp.zeros(o_ref.shape, o_ref.dtype)

    if epilogue == "swiglu":
        lhs_ref, w1_ref, w3_ref, o_ref, wb1_ref, wb3_ref = refs

        @pl.when(jnp.logical_and(valid, first))
        def _():
            _cast_weight(w1_ref, wb1_ref)
            _cast_weight(w3_ref, wb3_ref)

        @pl.when(valid)
        def _():
            lhs = lhs_ref[...]
            a = jnp.dot(lhs, wb1_ref[...], preferred_element_type=F32)
            b = jnp.dot(lhs, wb3_ref[...], preferred_element_type=F32)
            o_ref[...] = (_silu(a) * b).astype(o_ref.dtype)
        return

    if epilogue == "plain":
        lhs_ref, w_ref, o_ref, wb_ref = refs
    elif epilogue == "sigmoid_bias":
        lhs_ref, w_ref, bias_ref, o_ref, wb_ref = refs
    elif epilogue == "residual":
        lhs_ref, w_ref, x_ref, mod_ref, o_ref, wb_ref = refs
    else:
        raise ValueError(epilogue)

    @pl.when(jnp.logical_and(valid, first))
    def _():
        _cast_weight(w_ref, wb_ref)

    @pl.when(valid)
    def _():
        acc = jnp.dot(lhs_ref[...], wb_ref[...], preferred_element_type=F32)
        if epilogue == "sigmoid_bias":
            acc = _sigmoid(acc + bias_ref[...])
        elif epilogue == "residual":
            acc = x_ref[...] + mod_ref[g_idx:g_idx + 1, :] * acc
        o_ref[...] = acc.astype(o_ref.dtype)


def _gmm(lhs, weights, tile_group, n_tiles, *, tm, tn, epilogue, out_dtype,
         bias=None, xres=None, mod3=None, g_idx=0, mp=0, ds=1, name="gmm"):
    rows, k = lhs.shape
    n = weights[0].shape[-1]
    assert rows % tm == 0 and n % tn == 0 and k % 512 == 0
    n_row_tiles = rows // tm

    def row_of(t, nt):
        return jnp.minimum(t, nt[0] - 1)

    lhs_spec = pl.BlockSpec((tm, k), lambda j, t, tg, nt: (row_of(t, nt), 0))
    w_spec = pl.BlockSpec((None, k, tn), lambda j, t, tg, nt: (tg[row_of(t, nt)], 0, j))
    out_spec = pl.BlockSpec((tm, tn), lambda j, t, tg, nt: (t, j))
    in_specs = [lhs_spec] + [w_spec] * len(weights)
    args = [lhs] + list(weights)
    if epilogue == "sigmoid_bias":
        in_specs.append(pl.BlockSpec((1, tn), lambda j, t, tg, nt: (0, j)))
        args.append(bias.reshape(1, n))
    elif epilogue == "residual":
        in_specs.append(pl.BlockSpec((tm, tn), lambda j, t, tg, nt: (row_of(t, nt), j)))
        in_specs.append(pl.BlockSpec(
            (None, 6, tn), lambda j, t, tg, nt: (_segment(row_of(t, nt) * tm, mp, ds), 0, j)))
        args += [xres, mod3]
    scratch = [pltpu.VMEM((k, tn), BF16) for _ in weights]
    return pl.pallas_call(
        functools.partial(_gmm_kernel, epilogue=epilogue, g_idx=g_idx),
        grid_spec=pltpu.PrefetchScalarGridSpec(
            num_scalar_prefetch=2,
            grid=(n // tn, n_row_tiles),
            in_specs=in_specs,
            out_specs=out_spec,
            scratch_shapes=scratch,
        ),
        out_shape=jax.ShapeDtypeStruct((rows, n), out_dtype),
        compiler_params=_params("arbitrary", "arbitrary"),
        name=name,
    )(tile_group, n_tiles, *args)


def _dense_tables(rows, tm, group):
    nt = rows // tm
    return jnp.full((nt,), group, jnp.int32), jnp.full((1,), nt, jnp.int32)


def _qk_prep_kernel(q_ref, k_ref, v_ref, qg_ref, kg_ref, cos_ref, sa_ref, sb_ref,
                    qn_ref, kn_ref, kc_ref, vc_ref, *, hd, n_prompt_tiles):
    cos, sa, sb = cos_ref[...], sa_ref[...], sb_ref[...]

    def norm(xh, g):
        ms = jnp.mean(xh * xh, axis=-1, keepdims=True)
        return xh * lax.rsqrt(ms + EPS) * g

    def rope(xh):
        return xh * cos + pltpu.roll(xh, 3 * hd // 4, 1) * sa + pltpu.roll(xh, hd // 4, 1) * sb

    qg, kg = qg_ref[...], kg_ref[...]
    for h in range(q_ref.shape[1] // hd):
        xh = norm(q_ref[:, h * hd:(h + 1) * hd].astype(F32), qg)
        qn_ref[:, h * hd:(h + 1) * hd] = rope(xh).astype(qn_ref.dtype)
    is_prompt = pl.program_id(0) < n_prompt_tiles
    for h in range(k_ref.shape[1] // hd):
        xh = norm(k_ref[:, h * hd:(h + 1) * hd].astype(F32), kg)
        kn_ref[:, h * hd:(h + 1) * hd] = rope(xh).astype(kn_ref.dtype)

        @pl.when(is_prompt)
        def _():
            kc_ref[:, h * hd:(h + 1) * hd] = xh

    @pl.when(is_prompt)
    def _():
        vc_ref[...] = v_ref[...].astype(F32)


def _qk_prep(proj, q_off, k_off, v_off, q_cols, kv_cols, qg, kg, cos, sa, sb, mp, hd):
    m = proj.shape[0]
    tm = ROW_TILE
    npt = mp // tm
    row = lambda i: (i, 0)
    prow = lambda i: (jnp.minimum(i, npt - 1), 0)
    return pl.pallas_call(
        functools.partial(_qk_prep_kernel, hd=hd, n_prompt_tiles=npt),
        grid=(m // tm,),
        in_specs=[
            pl.BlockSpec((tm, q_cols), lambda i: (i, q_off // q_cols)),
            pl.BlockSpec((tm, kv_cols), lambda i: (i, k_off // kv_cols)),
            pl.BlockSpec((tm, kv_cols), lambda i: (i, v_off // kv_cols)),
            pl.BlockSpec((1, hd), lambda i: (0, 0)),
            pl.BlockSpec((1, hd), lambda i: (0, 0)),
            pl.BlockSpec((tm, hd), row),
            pl.BlockSpec((tm, hd), row),
            pl.BlockSpec((tm, hd), row),
        ],
        out_specs=[
            pl.BlockSpec((tm, q_cols), row),
            pl.BlockSpec((tm, kv_cols), row),
            pl.BlockSpec((tm, kv_cols), prow),
            pl.BlockSpec((tm, kv_cols), prow),
        ],
        out_shape=[
            jax.ShapeDtypeStruct((m, q_cols), BF16),
            jax.ShapeDtypeStruct((m, kv_cols), BF16),
            jax.ShapeDtypeStruct((mp, kv_cols), F32),
            jax.ShapeDtypeStruct((mp, kv_cols), F32),
        ],
        compiler_params=_params("arbitrary"),
        name="qk_prep",
    )(proj, proj, proj, qg.reshape(1, hd), kg.reshape(1, hd), cos, sa, sb)


def _rope_tables(mp, db, ds, hd):
    nf = hd // 4
    pos = jnp.arange(ds, dtype=jnp.int32)
    row = (pos // GRID_W).astype(F32)
    col = (pos % GRID_W).astype(F32)
    inv = ROPE_THETA ** (-jnp.arange(nf, dtype=F32) / nf)
    ang_row = row[:, None] * inv
    ang_col = col[:, None] * inv
    ang = jnp.concatenate([ang_row, ang_row, ang_col, ang_col], axis=1)
    cos, sin = jnp.cos(ang), jnp.sin(ang)
    lane = jnp.arange(hd) % (2 * nf)
    sa = jnp.where(lane < nf, -sin, 0.0)
    sb = jnp.where(lane >= nf, sin, 0.0)
    ones = jnp.ones((mp, hd), F32)
    zeros = jnp.zeros((mp, hd), F32)
    tile = lambda a: jnp.tile(a, (db, 1))
    return (jnp.concatenate([ones, tile(cos)]), jnp.concatenate([zeros, tile(sa)]),
            jnp.concatenate([zeros, tile(sb)]))


def _nt_dot(a, b):
    return lax.dot_general(a, b, (((1,), (1,)), ((), ())), preferred_element_type=F32)


def _attn_ctx_kernel(sink_ref, q_ref, k_ref, v_ref, o_ref, *, group, hd):
    h = pl.program_id(1)
    k = k_ref[...]
    v = v_ref[...]
    scale = hd ** -0.5
    for g in range(group):
        s = _nt_dot(q_ref[:, g * hd:(g + 1) * hd], k) * scale
        sink = sink_ref[h * group + g]
        mx = jnp.maximum(jnp.max(s, axis=-1, keepdims=True), sink)
        p = jnp.exp(s - mx)
        den = jnp.sum(p, axis=-1, keepdims=True) + jnp.exp(sink - mx)
        o = jnp.dot(p.astype(BF16), v, preferred_element_type=F32) / den
        o_ref[:, g * hd:(g + 1) * hd] = o.astype(o_ref.dtype)


def _attn_ctx(sink, qn, kn, proj, v_off, batch, seq, kvh, group, hd):
    gw = group * hd
    return pl.pallas_call(
        functools.partial(_attn_ctx_kernel, group=group, hd=hd),
        grid=(batch, kvh),
        in_specs=[
            pl.BlockSpec(memory_space=pltpu.SMEM),
            pl.BlockSpec((seq, gw), lambda b, h: (b, h)),
            pl.BlockSpec((seq, hd), lambda b, h: (b, h)),
            pl.BlockSpec((seq, hd), lambda b, h: (b, v_off // hd + h)),
        ],
        out_specs=pl.BlockSpec((seq, gw), lambda b, h: (b, h)),
        out_shape=jax.ShapeDtypeStruct((batch * seq, kvh * gw), BF16),
        compiler_params=_params("arbitrary", "arbitrary"),
        name="attn_ctx",
    )(sink, qn, kn, proj)


def _attn_lat_kernel(sink_ref, q_ref, kp_ref, kc_ref, kx_ref, vp_ref, vc_ref, vx_ref,
                     ck_ref, cv_ref, o_ref, *, group, hd, tq):
    h = pl.program_id(1)
    i = pl.program_id(2)
    has_prev = i > 0
    has_next = i < pl.num_programs(2) - 1
    scale = hd ** -0.5
    r = lax.broadcasted_iota(jnp.int32, (tq, WINDOW), 0)
    c = lax.broadcasted_iota(jnp.int32, (tq, WINDOW), 1)
    mask_prev = jnp.logical_and(c >= r, has_prev)
    mask_next = jnp.logical_and(tq + c - r <= WINDOW, has_next)
    rc = lax.broadcasted_iota(jnp.int32, (tq, tq), 0)
    cc = lax.broadcasted_iota(jnp.int32, (tq, tq), 1)
    mask_cur = jnp.abs(cc - rc) <= WINDOW
    kp, kc, kx = kp_ref[...], kc_ref[...], kx_ref[...]
    vp, vc, vx = vp_ref[...], vc_ref[...], vx_ref[...]
    ck = ck_ref[...].astype(BF16)
    cv = cv_ref[...].astype(BF16)
    for g in range(group):
        q = q_ref[:, g * hd:(g + 1) * hd]
        sp = jnp.where(mask_prev, _nt_dot(q, kp) * scale, NEG_BIG)
        sc = jnp.where(mask_cur, _nt_dot(q, kc) * scale, NEG_BIG)
        sx = jnp.where(mask_next, _nt_dot(q, kx) * scale, NEG_BIG)
        sk = _nt_dot(q, ck) * scale
        sink = sink_ref[h * group + g]
        mx = jnp.maximum(
            jnp.maximum(jnp.max(sp, axis=-1, keepdims=True), jnp.max(sc, axis=-1, keepdims=True)),
            jnp.maximum(jnp.max(sx, axis=-1, keepdims=True), jnp.max(sk, axis=-1, keepdims=True)))
        mx = jnp.maximum(mx, sink)
        pp, pc, px, pk = jnp.exp(sp - mx), jnp.exp(sc - mx), jnp.exp(sx - mx), jnp.exp(sk - mx)
        den = (jnp.sum(pp, axis=-1, keepdims=True) + jnp.sum(pc, axis=-1, keepdims=True)
               + jnp.sum(px, axis=-1, keepdims=True) + jnp.sum(pk, axis=-1, keepdims=True)
               + jnp.exp(sink - mx))
        o = (jnp.dot(pp.astype(BF16), vp, preferred_element_type=F32)
             + jnp.dot(pc.astype(BF16), vc, preferred_element_type=F32)
             + jnp.dot(px.astype(BF16), vx, preferred_element_type=F32)
             + jnp.dot(pk.astype(BF16), cv, preferred_element_type=F32)) / den
        o_ref[:, g * hd:(g + 1) * hd] = o.astype(o_ref.dtype)


def _attn_lat(sink, qn, kn, proj, v_off, ctx_k, ctx_v, layer, mp, db, ds, kvh, group, hd):
    gw = group * hd
    tq = ROW_TILE
    assert tq >= WINDOW and tq % WINDOW == 0 and ds % tq == 0
    past = ctx_k.shape[3]
    per = tq // WINDOW
    nwb = ds // WINDOW

    def cur(b, h, i):
        return (mp // tq + b * (ds // tq) + i, h)

    def prev(b, h, i):
        return (mp // WINDOW + b * nwb + jnp.maximum(i * per - 1, 0), h)

    def nxt(b, h, i):
        return (mp // WINDOW + b * nwb + jnp.minimum((i + 1) * per, nwb - 1), h)

    def vshift(f):
        return lambda b, h, i: (f(b, h, i)[0], v_off // hd + h)

    ctx_spec = pl.BlockSpec((None, None, None, past, hd), lambda b, h, i: (b, layer, h, 0, 0))
    return pl.pallas_call(
        functools.partial(_attn_lat_kernel, group=group, hd=hd, tq=tq),
        grid=(db, kvh, ds // tq),
        in_specs=[
            pl.BlockSpec(memory_space=pltpu.SMEM),
            pl.BlockSpec((tq, gw), cur),
            pl.BlockSpec((WINDOW, hd), prev),
            pl.BlockSpec((tq, hd), cur),
            pl.BlockSpec((WINDOW, hd), nxt),
            pl.BlockSpec((WINDOW, hd), vshift(prev)),
            pl.BlockSpec((tq, hd), vshift(cur)),
            pl.BlockSpec((WINDOW, hd), vshift(nxt)),
            ctx_spec,
            ctx_spec,
        ],
        out_specs=pl.BlockSpec((tq, gw), lambda b, h, i: (b * (ds // tq) + i, h)),
        out_shape=jax.ShapeDtypeStruct((db * ds, kvh * gw), BF16),
        compiler_params=_params("arbitrary", "arbitrary", "arbitrary"),
        name="attn_lat",
    )(sink, qn, kn, kn, kn, proj, proj, proj, ctx_k, ctx_v)


def _seq_edges(i, tm, mp, seq, ds):
    r0 = i * tm
    in_prompt = r0 < mp
    local0 = jnp.where(in_prompt, r0 % seq, (r0 - mp) % ds)
    length = jnp.where(in_prompt, seq, ds)
    return local0 == 0, local0 + tm == length


def _halo_specs(tm, width, col, n_rows):
    per = tm // HALO
    last = n_rows // HALO - 1
    return (
        pl.BlockSpec((HALO, width), lambda i, *c: (jnp.maximum(i * per - 1, 0), col(*c))),
        pl.BlockSpec((tm, width), lambda i, *c: (i, col(*c))),
        pl.BlockSpec((HALO, width), lambda i, *c: (jnp.minimum((i + 1) * per, last), col(*c))),
    )


def _branch_a_kernel(vp_ref, vc_ref, vn_ref, gp_ref, gc_ref, gn_ref, w_ref, b_ref, lg_ref, lb_ref,
                     o_ref, ext_ref, conv_ref, *, tm, taps, mp, seq, ds):
    starts, ends = _seq_edges(pl.program_id(0), tm, mp, seq, ds)
    keep_prev = jnp.where(starts, 0.0, 1.0)
    keep_next = jnp.where(ends, 0.0, 1.0)

    def glu(v_ref, g_ref):
        return v_ref[...].astype(F32) * _sigmoid(g_ref[...].astype(F32))

    ext_ref[0:HALO, :] = glu(vp_ref, gp_ref) * keep_prev
    ext_ref[HALO:HALO + tm, :] = glu(vc_ref, gc_ref)
    ext_ref[HALO + tm:, :] = glu(vn_ref, gn_ref) * keep_next

    pad = taps // 2
    rc = 64
    width = o_ref.shape[1]
    for c0 in range(0, width, LANES):
        for r0 in range(0, tm, rc):
            acc = jnp.zeros((rc, LANES), F32)
            for k in range(taps):
                start = HALO - pad + k + r0
                acc = acc + w_ref[k:k + 1, c0:c0 + LANES] * ext_ref[start:start + rc, c0:c0 + LANES]
            conv_ref[r0:r0 + rc, c0:c0 + LANES] = acc + b_ref[:, c0:c0 + LANES]

    y = conv_ref[...]
    mu = jnp.mean(y, axis=-1, keepdims=True)
    yc = y - mu
    var = jnp.mean(yc * yc, axis=-1, keepdims=True)
    z = yc * lax.rsqrt(var + EPS) * lg_ref[...] + lb_ref[...]
    o_ref[...] = _silu(z).astype(o_ref.dtype)


def _branch_a(proj, val_off, gate_off, width, conv_w, conv_b, ln_g, ln_b, mp, seq, ds):
    m = proj.shape[0]
    tm = ROW_TILE
    taps = conv_w.shape[0]
    assert taps // 2 <= HALO
    vec = pl.BlockSpec((1, width), lambda i: (0, 0))
    return pl.pallas_call(
        functools.partial(_branch_a_kernel, tm=tm, taps=taps, mp=mp, seq=seq, ds=ds),
        grid=(m // tm,),
        in_specs=[
            *_halo_specs(tm, width, lambda: val_off // width, m),
            *_halo_specs(tm, width, lambda: gate_off // width, m),
            pl.BlockSpec((taps, width), lambda i: (0, 0)),
            vec, vec, vec,
        ],
        out_specs=pl.BlockSpec((tm, width), lambda i: (i, 0)),
        out_shape=jax.ShapeDtypeStruct((m, width), BF16),
        scratch_shapes=[pltpu.VMEM((tm + 2 * HALO, width), F32), pltpu.VMEM((tm, width), F32)],
        compiler_params=_params("arbitrary"),
        name="branch_a",
    )(proj, proj, proj, proj, proj, proj, conv_w, conv_b.reshape(1, width),
      ln_g.reshape(1, width), ln_b.reshape(1, width))


def _branch_d_kernel(xp_ref, xc_ref, xn_ref, cp_ref, cc_ref, cn_ref, b_ref, w_ref,
                     o_ref, ext_ref, *, tm, taps, mp, seq, ds):
    starts, ends = _seq_edges(pl.program_id(0), tm, mp, seq, ds)
    keep_prev = jnp.where(starts, 0.0, 1.0)
    keep_next = jnp.where(ends, 0.0, 1.0)

    def prod(c_ref, x_ref):
        return c_ref[...].astype(F32) * x_ref[...].astype(F32)

    ext_ref[0:HALO, :] = prod(cp_ref, xp_ref) * keep_prev
    ext_ref[HALO:HALO + tm, :] = prod(cc_ref, xc_ref)
    ext_ref[HALO + tm:, :] = prod(cn_ref, xn_ref) * keep_next
    pad = taps // 2
    acc = jnp.zeros(o_ref.shape, F32)
    for k in range(taps):
        start = HALO - pad + k
        acc = acc + w_ref[k:k + 1, :] * ext_ref[start:start + tm, :]
    o_ref[...] = (b_ref[...].astype(F32) * acc).astype(o_ref.dtype)


def _branch_d(proj, x_off, b_off, c_off, width, conv_w, mp, seq, ds):
    m = proj.shape[0]
    tm = ROW_TILE
    tw = 512
    taps = conv_w.shape[0]
    return pl.pallas_call(
        functools.partial(_branch_d_kernel, tm=tm, taps=taps, mp=mp, seq=seq, ds=ds),
        grid=(m // tm, width // tw),
        in_specs=[
            *_halo_specs(tm, tw, lambda c: x_off // tw + c, m),
            *_halo_specs(tm, tw, lambda c: c_off // tw + c, m),
            pl.BlockSpec((tm, tw), lambda i, c: (i, b_off // tw + c)),
            pl.BlockSpec((taps, tw), lambda i, c: (0, c)),
        ],
        out_specs=pl.BlockSpec((tm, tw), lambda i, c: (i, c)),
        out_shape=jax.ShapeDtypeStruct((m, width), BF16),
        scratch_shapes=[pltpu.VMEM((tm + 2 * HALO, tw), F32)],
        compiler_params=_params("arbitrary", "arbitrary"),
        name="branch_d",
    )(proj, proj, proj, proj, proj, proj, proj, conv_w)


def _dft_mats(n):
    k = jnp.arange(n, dtype=jnp.int32)
    ang = ((k[:, None] * k[None, :]) % n).astype(F32) * (2.0 * np.pi / n)
    return jnp.cos(ang).astype(BF16), jnp.sin(ang).astype(BF16)


def _fnet_chan_kernel(lo_ref, hi_ref, cc_ref, sc_ref, u_ref, v_ref, *, gw):
    cc, sc = cc_ref[...], sc_ref[...]
    half = lo_ref.shape[1]
    for g in range(2 * half // gw):
        src = lo_ref if g * gw < half else hi_ref
        o = g * gw % half
        x = src[:, o:o + gw]
        u_ref[:, g * gw:(g + 1) * gw] = jnp.dot(x, cc, preferred_element_type=F32).astype(u_ref.dtype)
        v_ref[:, g * gw:(g + 1) * gw] = jnp.dot(x, sc, preferred_element_type=F32).astype(v_ref.dtype)


def _fnet_chan(proj, f_off, width, cc, sc):
    m = proj.shape[0]
    tm = 512
    half = width // 2
    gw = width // N_FOURIER_GROUPS
    mat = pl.BlockSpec((gw, gw), lambda i: (0, 0))
    out = pl.BlockSpec((tm, width), lambda i: (i, 0))
    return pl.pallas_call(
        functools.partial(_fnet_chan_kernel, gw=gw),
        grid=(m // tm,),
        in_specs=[
            pl.BlockSpec((tm, half), lambda i: (i, f_off // half)),
            pl.BlockSpec((tm, half), lambda i: (i, f_off // half + 1)),
            mat, mat,
        ],
        out_specs=[out, out],
        out_shape=[jax.ShapeDtypeStruct((m, width), BF16)] * 2,
        compiler_params=_params("arbitrary"),
        name="fnet_chan",
    )(proj, proj, cc, sc)


def _fnet_seq_kernel(cs_ref, ss_ref, u_ref, v_ref, o_ref, *, scale):
    acc = (jnp.dot(cs_ref[...], u_ref[...], preferred_element_type=F32)
           - jnp.dot(ss_ref[...], v_ref[...], preferred_element_type=F32))
    o_ref[...] = (acc * scale).astype(o_ref.dtype)


def _fnet_seq(u, v, cs, ss, row0, batch, seq, gw, name):
    width = u.shape[1]
    tm = min(seq, 512)
    tn = 512
    scale = float((seq * gw) ** -0.5)
    lhs = pl.BlockSpec((tm, seq), lambda b, j, i: (i, 0))
    rhs = pl.BlockSpec((seq, tn), lambda b, j, i: (row0 // seq + b, j))
    return pl.pallas_call(
        functools.partial(_fnet_seq_kernel, scale=scale),
        grid=(batch, width // tn, seq // tm),
        in_specs=[lhs, lhs, rhs, rhs],
        out_specs=pl.BlockSpec((tm, tn), lambda b, j, i: (b * (seq // tm) + i, j)),
        out_shape=jax.ShapeDtypeStruct((batch * seq, width), BF16),
        compiler_params=_params("arbitrary", "arbitrary", "arbitrary"),
        name=name,
    )(cs, ss, u, v)


def _merge_kernel(*refs, nb):
    br = refs[:nb]
    gt = refs[nb:2 * nb]
    w_ref, o_ref, wb_ref = refs[2 * nb:]

    @pl.when(pl.program_id(1) == 0)
    def _():
        for n in range(nb):
            _cast_weight(w_ref.at[n], wb_ref.at[n])

    acc = jnp.zeros(o_ref.shape, F32)
    for n in range(nb):
        acc = acc + gt[n][...].astype(F32) * jnp.dot(br[n][...], wb_ref[n], preferred_element_type=F32)
    o_ref[...] = acc.astype(o_ref.dtype)


def _merge(branches, gates, w_branch, layer):
    _, nb, k, d = w_branch.shape
    m = branches[0].shape[0]
    tm, tn = 512, 512
    b_spec = pl.BlockSpec((tm, k), lambda j, i: (i, 0))
    g_specs = [pl.BlockSpec((tm, tn), functools.partial(lambda j, i, n: (i, n * (d // tn) + j), n=n))
               for n in range(nb)]
    return pl.pallas_call(
        functools.partial(_merge_kernel, nb=nb),
        grid=(d // tn, m // tm),
        in_specs=[b_spec] * nb + g_specs + [pl.BlockSpec((None, nb, k, tn), lambda j, i: (layer, 0, 0, j))],
        out_specs=pl.BlockSpec((tm, tn), lambda j, i: (i, j)),
        out_shape=jax.ShapeDtypeStruct((m, d), BF16),
        scratch_shapes=[pltpu.VMEM((nb, k, tn), BF16)],
        compiler_params=_params("arbitrary", "arbitrary"),
        name="merge",
    )(*branches, *([gates] * nb), w_branch)


def _router_kernel(h_ref, w_ref, b_ref, p_ref, e_ref):
    logits = jnp.dot(h_ref[...], w_ref[...], preferred_element_type=F32,
                     precision=lax.Precision.HIGHEST) + b_ref[...]
    lane = lax.broadcasted_iota(jnp.int32, logits.shape, 1)
    v1 = jnp.max(logits, axis=-1, keepdims=True)
    i1 = jnp.min(jnp.where(logits == v1, lane, LANES), axis=-1, keepdims=True)
    rest = jnp.where(lane == i1, NEG_BIG, logits)
    v2 = jnp.max(rest, axis=-1, keepdims=True)
    i2 = jnp.min(jnp.where(rest == v2, lane, LANES), axis=-1, keepdims=True)
    e = jnp.exp(v2 - v1)
    p1 = 1.0 / (1.0 + e)
    p2 = e / (1.0 + e)
    p_ref[...] = jnp.where(lane == 0, p1, jnp.where(lane == 1, p2, 0.0))
    e_ref[...] = jnp.where(lane == 0, i1, jnp.where(lane == 1, i2, 0))


def _router(h32, router_w, router_b):
    m, d = h32.shape
    ne = router_w.shape[1]
    tm = ROW_TILE
    w = jnp.zeros((d, LANES), F32).at[:, :ne].set(router_w)
    b = jnp.full((1, LANES), NEG_BIG, F32).at[0, :ne].set(router_b)
    out = pl.BlockSpec((tm, LANES), lambda i: (i, 0))
    return pl.pallas_call(
        _router_kernel,
        grid=(m // tm,),
        in_specs=[
            pl.BlockSpec((tm, d), lambda i: (i, 0)),
            pl.BlockSpec((d, LANES), lambda i: (0, 0)),
            pl.BlockSpec((1, LANES), lambda i: (0, 0)),
        ],
        out_specs=[out, out],
        out_shape=[jax.ShapeDtypeStruct((m, LANES), F32), jax.ShapeDtypeStruct((m, LANES), jnp.int32)],
        compiler_params=_params("arbitrary"),
        name="router",
    )(h32, w, b)


def _routing_tables(e1, e2, n_experts, tile):
    m = e1.shape[0]
    tok = jnp.arange(m, dtype=jnp.int32)
    ids = jnp.arange(n_experts, dtype=jnp.int32)
    sel = ((e1[:, None] == ids) | (e2[:, None] == ids)).astype(jnp.int32)
    rank = jnp.cumsum(sel, axis=0) - sel
    counts = jnp.sum(sel, axis=0)
    padded = (counts + tile - 1) // tile * tile
    ends = jnp.cumsum(padded)
    offs = ends - padded
    slot = offs[None, :] + rank
    pos1 = jnp.sum(jnp.where(e1[:, None] == ids, slot, 0), axis=1)
    pos2 = jnp.sum(jnp.where(e2[:, None] == ids, slot, 0), axis=1)
    n_rows = TOP_K * m + n_experts * tile
    row_src = jnp.zeros((n_rows,), jnp.int32).at[pos1].set(tok).at[pos2].set(tok)
    tile_start = jnp.arange(n_rows // tile, dtype=jnp.int32) * tile
    tile_group = jnp.minimum(jnp.sum(ends[None, :] <= tile_start[:, None], axis=1), n_experts - 1)
    n_tiles = (ends[-1] // tile).reshape(1)
    return pos1, pos2, row_src, tile_group.astype(jnp.int32), n_tiles.astype(jnp.int32)


def _gather_kernel(src_ref, h_hbm, o_ref, buf_ref, sem, *, tile):
    base = pl.program_id(0) * tile

    def row_copy(r):
        return pltpu.make_async_copy(
            h_hbm.at[pl.ds(src_ref[base + r], 1), :], buf_ref.at[pl.ds(r, 1), :], sem)

    def start(r, carry):
        row_copy(r).start()
        return carry

    def wait(r, carry):
        row_copy(r).wait()
        return carry

    lax.fori_loop(0, tile, start, 0)
    lax.fori_loop(0, tile, wait, 0)
    o_ref[...] = buf_ref[...].astype(o_ref.dtype)


def _gather_rows(h32, row_src, tile):
    d = h32.shape[1]
    n_rows = row_src.shape[0]
    return pl.pallas_call(
        functools.partial(_gather_kernel, tile=tile),
        grid_spec=pltpu.PrefetchScalarGridSpec(
            num_scalar_prefetch=1,
            grid=(n_rows // tile,),
            in_specs=[pl.BlockSpec(memory_space=pl.ANY)],
            out_specs=pl.BlockSpec((tile, d), lambda t, src: (t, 0)),
            scratch_shapes=[pltpu.VMEM((tile, d), F32), pltpu.SemaphoreType.DMA(())],
        ),
        out_shape=jax.ShapeDtypeStruct((n_rows, d), BF16),
        compiler_params=_params("arbitrary"),
        name="moe_gather",
    )(row_src, h32)


def _combine_kernel(p1_ref, p2_ref, y_hbm, x_ref, w_ref, mod_ref, o_ref, y1_ref, y2_ref, sems,
                    *, tile, g_idx):
    base = pl.program_id(0) * tile

    def copies(r):
        return (
            pltpu.make_async_copy(y_hbm.at[pl.ds(p1_ref[base + r], 1), :],
                                  y1_ref.at[pl.ds(r, 1), :], sems.at[0]),
            pltpu.make_async_copy(y_hbm.at[pl.ds(p2_ref[base + r], 1), :],
                                  y2_ref.at[pl.ds(r, 1), :], sems.at[1]),
        )

    def start(r, carry):
        for cp in copies(r):
            cp.start()
        return carry

    def wait(r, carry):
        for cp in copies(r):
            cp.wait()
        return carry

    lax.fori_loop(0, tile, start, 0)
    lax.fori_loop(0, tile, wait, 0)
    w = w_ref[...]
    ffn = w[:, 0:1] * y1_ref[...] + w[:, 1:2] * y2_ref[...]
    o_ref[...] = x_ref[...] + mod_ref[g_idx:g_idx + 1, :] * ffn


def _combine(pos1, pos2, ys, x, pw, mod3, g_idx, mp, ds):
    m, d = x.shape
    tile = ROW_TILE
    return pl.pallas_call(
        functools.partial(_combine_kernel, tile=tile, g_idx=g_idx),
        grid_spec=pltpu.PrefetchScalarGridSpec(
            num_scalar_prefetch=2,
            grid=(m // tile,),
            in_specs=[
                pl.BlockSpec(memory_space=pl.ANY),
                pl.BlockSpec((tile, d), lambda i, a, b: (i, 0)),
                pl.BlockSpec((tile, LANES), lambda i, a, b: (i, 0)),
                pl.BlockSpec((None, 6, d), lambda i, a, b: (_segment(i * tile, mp, ds), 0, 0)),
            ],
            out_specs=pl.BlockSpec((tile, d), lambda i, a, b: (i, 0)),
            scratch_shapes=[pltpu.VMEM((tile, d), F32), pltpu.VMEM((tile, d), F32),
                            pltpu.SemaphoreType.DMA((2,))],
        ),
        out_shape=jax.ShapeDtypeStruct((m, d), F32),
        compiler_params=_params("arbitrary"),
        name="moe_combine",
    )(pos1, pos2, ys, x, pw, mod3)


def _mixer(x, mod3, layer, dims, tables, ctx_k, ctx_v, norm_g, w_in, conv_a_w, conv_a_b, ln_a_g, ln_a_b,
           q_norm_g, k_norm_g, attn_sink, conv_d_w, w_branch, w_gate, b_gate, w_o):
    mp, ds, batch, seq, db, kvh, hd = dims
    m, d = x.shape
    bw = w_branch.shape[2]
    q_cols = bw
    kv_cols = kvh * hd
    group = q_cols // hd // kvh
    offs = np.cumsum([0, bw, bw, q_cols, kv_cols, kv_cols, bw, bw, bw, bw]).tolist()
    a_val, a_gate, q_off, k_off, v_off, f_off, dx_off, db_off, dc_off, n_in = offs
    assert w_in.shape[2] == n_in
    cos, sa, sb, cc, sc, cs_p, ss_p, cs_s, ss_s = tables

    h = _norm_mod(x, norm_g, mod3, 0, 1, BF16, mp, ds)
    tg, nt = _dense_tables(m, 512, layer)
    proj = _gmm(h, [w_in], tg, nt, tm=512, tn=512, epilogue="plain", out_dtype=BF16, name="in_proj")
    gates = _gmm(h, [w_gate], tg, nt, tm=512, tn=512, epilogue="sigmoid_bias", out_dtype=BF16,
                 bias=b_gate[layer], name="gate_proj")

    a = _branch_a(proj, a_val, a_gate, bw, conv_a_w, conv_a_b, ln_a_g, ln_a_b, mp, seq, ds)
    qn, kn, k_new, v_new = _qk_prep(proj, q_off, k_off, v_off, q_cols, kv_cols, q_norm_g, k_norm_g,
                                    cos, sa, sb, mp, hd)
    att_p = _attn_ctx(attn_sink, qn, kn, proj, v_off, batch, seq, kvh, group, hd)
    att_s = _attn_lat(attn_sink, qn, kn, proj, v_off, ctx_k, ctx_v, layer, mp, db, ds, kvh, group, hd)
    att = jnp.concatenate([att_p, att_s], axis=0)
    u, v = _fnet_chan(proj, f_off, bw, cc, sc)
    gw = bw // N_FOURIER_GROUPS
    four = jnp.concatenate([
        _fnet_seq(u, v, cs_p, ss_p, 0, batch, seq, gw, "fnet_seq_ctx"),
        _fnet_seq(u, v, cs_s, ss_s, mp, db, ds, gw, "fnet_seq_lat"),
    ], axis=0)
    dy = _branch_d(proj, dx_off, db_off, dc_off, bw, conv_d_w, mp, seq, ds)

    merged = _merge([a, att, four, dy], gates, w_branch, layer)
    x = _gmm(merged, [w_o], tg, nt, tm=512, tn=512, epilogue="residual", out_dtype=F32,
             xres=x, mod3=mod3, g_idx=2, mp=mp, ds=ds, name="out_proj")
    return x, k_new, v_new


def _dense_ffn(x, mod3, dims, norm_g, w1, w3, w2, j):
    mp, ds = dims[0], dims[1]
    m = x.shape[0]
    h = _norm_mod(x, norm_g, mod3, 3, 4, BF16, mp, ds)
    tg, nt = _dense_tables(m, 512, j)
    act = _gmm(h, [w1, w3], tg, nt, tm=512, tn=512, epilogue="swiglu", out_dtype=BF16,
               name="ffn_up")
    return _gmm(act, [w2], tg, nt, tm=512, tn=256, epilogue="residual", out_dtype=F32,
                xres=x, mod3=mod3, g_idx=5, mp=mp, ds=ds, name="ffn_down")


def _moe_ffn(x, mod3, dims, norm_g, router_w, router_b, w1, w3, w2, j):
    mp, ds = dims[0], dims[1]
    n_experts = w1.shape[1]
    w1, w3, w2 = (w.reshape((-1,) + w.shape[2:]) for w in (w1, w3, w2))
    h32 = _norm_mod(x, norm_g, mod3, 3, 4, F32, mp, ds)
    pw, ew = _router(h32, router_w, router_b)
    pos1, pos2, row_src, tile_group, n_tiles = _routing_tables(ew[:, 0], ew[:, 1], n_experts, MOE_TILE)
    tile_group = tile_group + j * n_experts
    hs = _gather_rows(h32, row_src, MOE_TILE)
    act = _gmm(hs, [w1, w3], tile_group, n_tiles, tm=MOE_TILE, tn=512, epilogue="swiglu",
               out_dtype=BF16, name="moe_up")
    ys = _gmm(act, [w2], tile_group, n_tiles, tm=MOE_TILE, tn=256, epilogue="plain",
              out_dtype=F32, name="moe_down")
    return _combine(pos1, pos2, ys, x, pw, mod3, 5, mp, ds)


def kernel(x_prompt, x_sample, c, cache_k, cache_v, c_ctx, ada_w, ada_b, norm_mix_g, norm_ffn_g,
           w_in, conv_a_w, conv_a_b, ln_a_g, ln_a_b, q_norm_g, k_norm_g, attn_sink, conv_d_w,
           w_branch, w_gate, b_gate, w_o, ffn_w1, ffn_w3, ffn_w2, router_w, router_b,
           moe_w1, moe_w3, moe_w2):
    batch, seq, d = x_prompt.shape
    db, ds, _ = x_sample.shape
    depth = ada_w.shape[0]
    kvh, hd = cache_k.shape[3], cache_k.shape[4]
    mp = batch * seq
    ms = db * ds
    assert seq == ROW_TILE and ds % ROW_TILE == 0 and mp % 512 == 0 and ms % 512 == 0
    dims = (mp, ds, batch, seq, db, kvh, hd)

    x = jnp.concatenate([x_prompt.reshape(mp, d), x_sample.reshape(ms, d)], axis=0)
    n_seg = 1 + db
    cvec = jnp.zeros((8, d), F32).at[0].set(c_ctx).at[1:n_seg].set(c)
    mod = _adaln(cvec, ada_w, ada_b)[:, :n_seg].reshape(depth, n_seg, 6, d)

    bw = w_branch.shape[2]
    gw = bw // N_FOURIER_GROUPS
    tables = (*_rope_tables(mp, db, ds, hd), *_dft_mats(gw), *_dft_mats(seq), *_dft_mats(ds))
    ctx_k = jnp.transpose(cache_k, (0, 1, 3, 2, 4))
    ctx_v = jnp.transpose(cache_v, (0, 1, 3, 2, 4))

    new_k, new_v = [], []
    for l in range(depth):
        x, k_l, v_l = _mixer(
            x, mod[l], l, dims, tables, ctx_k, ctx_v, norm_mix_g[l], w_in, conv_a_w[l], conv_a_b[l],
            ln_a_g[l], ln_a_b[l], q_norm_g[l], k_norm_g[l], attn_sink[l], conv_d_w[l], w_branch,
            w_gate, b_gate, w_o)
        new_k.append(k_l.reshape(batch, seq, kvh, hd))
        new_v.append(v_l.reshape(batch, seq, kvh, hd))
        j = l // 2
        if l % 2 == 1:
            x = _moe_ffn(x, mod[l], dims, norm_ffn_g[l], router_w[j], router_b[j],
                         moe_w1, moe_w3, moe_w2, j)
        else:
            x = _dense_ffn(x, mod[l], dims, norm_ffn_g[l], ffn_w1, ffn_w3, ffn_w2, j)

    y_prompt = x[:mp].reshape(batch, seq, d)
    y_sample = x[mp:].reshape(db, ds, d)
    return y_prompt, y_sample, jnp.stack(new_k, axis=1), jnp.stack(new_v, axis=1)
```
